```python
import math
import jax, jax.numpy as jnp
from jax import lax
import numpy as np

D_MODEL = 1024
BATCH = 8
SEQ = 2048
DEPTH = 4

GRID_W = 64
CTX_LEN = 256
HEAD_DIM = 64
ROPE_THETA = 10000.0
NORM_EPS = 1e-6
Q_BLOCK = 128
A_HEADS = 4
A_KV_HEADS = 2
B_HEADS = 4
B_QK_DIM = 32
B_V_DIM = 2 * B_QK_DIM
C_HEADS = 8
C_HEAD = 64
C_WIDTH = C_HEADS * C_HEAD
C_DECAY_LORA = 64
C_ICL_LORA = 64
C_GATE_LORA = 128
N_DIR = 2
C_GN_EPS = 64e-5
A_Q = A_HEADS * HEAD_DIM
A_KV = A_KV_HEADS * HEAD_DIM
B_Q = B_HEADS * 2 * B_QK_DIM
B_V = B_HEADS * B_V_DIM
C_SLICE = 3 * C_WIDTH + N_DIR * C_DECAY_LORA + N_DIR * C_ICL_LORA + C_GATE_LORA
IN_SPLITS = (A_Q, A_Q + A_KV, A_Q + 2 * A_KV, A_Q + 2 * A_KV + B_Q, A_Q + 2 * A_KV + 2 * B_Q, A_Q + 2 * A_KV + 2 * B_Q + B_V)
IN_WIDTH = A_Q + 2 * A_KV + 2 * B_Q + B_V + C_SLICE
MIX_WIDTH = A_HEADS * HEAD_DIM + B_HEADS * B_V_DIM + C_WIDTH
N_GROUPS = 4
EXPERTS_PER_GROUP = 8
N_EXPERTS = N_GROUPS * EXPERTS_PER_GROUP
TOP_K = 2
EXPERT_FF = 512
EXPERT_BLOCK = 128

kernel_name = "hymba_style_diff_rwkv7_hier_moe_dit"


def rms_norm(x, gain, eps=NORM_EPS):
    xf = x.astype(jnp.float32)
    y = xf * lax.rsqrt(jnp.mean(xf * xf, axis=-1, keepdims=True) + eps)
    return (y * gain.astype(jnp.float32)).astype(x.dtype)


def ada_norm(x, gain, shift, scale):
    return rms_norm(x, gain) * (1 + scale) + shift


def axial_rope_tables(row_idx, col_idx, dim):
    half = dim // 2
    inv = ROPE_THETA ** (-jnp.arange(0, half, 2, dtype=jnp.float32) / half)
    ar = row_idx.astype(jnp.float32)[:, None] * inv
    ac = col_idx.astype(jnp.float32)[:, None] * inv
    ang = jnp.concatenate([ar, ar, ac, ac], axis=-1)
    return jnp.cos(ang), jnp.sin(ang)


def rotate_halves(x):
    q = x.shape[-1] // 4
    x = x.reshape(x.shape[:-1] + (2, 2, q))
    x = jnp.stack([-x[..., 1, :], x[..., 0, :]], axis=-2)
    return x.reshape(x.shape[:-3] + (4 * q,))


def apply_rope(x, cos, sin):
    xf = x.astype(jnp.float32)
    return (xf * cos + rotate_halves(xf) * sin).astype(x.dtype)


def split_combined(p):
    bsz, t = p.shape[:2]
    aq, ak, av, bq, bk, bv, cp = jnp.split(p, list(IN_SPLITS), axis=-1)
    return (aq.reshape(bsz, t, A_HEADS, HEAD_DIM), ak.reshape(bsz, t, A_KV_HEADS, HEAD_DIM),
            av.reshape(bsz, t, A_KV_HEADS, HEAD_DIM), bq.reshape(bsz, t, B_HEADS, 2, B_QK_DIM),
            bk.reshape(bsz, t, B_HEADS, 2, B_QK_DIM), bv.reshape(bsz, t, B_HEADS, B_V_DIM), cp)


def sweep_query_blocks(attend, q, *args):
    b, s = q.shape[:2]
    nb = s // Q_BLOCK
    qb = jnp.moveaxis(q.reshape((b, nb, Q_BLOCK) + q.shape[2:]), 1, 0)
    out = lax.map(lambda blk: attend(blk, *args), qb)
    return jnp.moveaxis(out, 0, 1).reshape(b, s, -1)


def gqa_attend(q, k, v):
    b, t, h, d = q.shape
    kvh = k.shape[2]
    qg = q.reshape(b, t, kvh, h // kvh, d)
    s = jnp.einsum('btkgd,bskd->bkgts', qg, k, preferred_element_type=jnp.float32) * (d ** -0.5)
    p = jax.nn.softmax(s, axis=-1).astype(v.dtype)
    o = jnp.einsum('bkgts,bskd->btkgd', p, v)
    return o.reshape(b, t, h * d)


def diff_attend(q, k, v, lam):
    b, t = q.shape[:2]
    d = q.shape[-1]
    s = jnp.einsum('bthmd,bshmd->bhmts', q, k, preferred_element_type=jnp.float32) * (d ** -0.5)
    p = jax.nn.softmax(s, axis=-1)
    a = (p[:, :, 0] - lam * p[:, :, 1]).astype(v.dtype)
    o = jnp.einsum('bhts,bshe->bthe', a, v)
    return o.reshape(b, t, -1)


def diff_head_norm(o, gain, lam_init):
    b, t = o.shape[:2]
    o = rms_norm(o.reshape(b, t, B_HEADS, B_V_DIM), gain) * (1.0 - lam_init)
    return o.reshape(b, t, B_HEADS * B_V_DIM)


def rwkv_inputs(p, mu, w0, w2, a0, a2, g2, k_k, k_a):
    prev = jnp.pad(p[:, :-1], ((0, 0), (1, 0), (0, 0)))
    nxt = jnp.pad(p[:, 1:], ((0, 0), (0, 1), (0, 0)))
    pf = (p + mu * (0.5 * (prev + nxt) - p)).astype(jnp.float32)
    o1 = C_WIDTH; o2 = 2 * C_WIDTH; o3 = 3 * C_WIDTH
    o4 = o3 + N_DIR * C_DECAY_LORA; o5 = o4 + N_DIR * C_ICL_LORA
    r, k, v, wd, ad, gd = jnp.split(pf, [o1, o2, o3, o4, o5], axis=-1)
    bsz, t = p.shape[:2]
    wd = wd.reshape(bsz, t, N_DIR, C_DECAY_LORA)
    ad = ad.reshape(bsz, t, N_DIR, C_ICL_LORA)
    wlog = -jax.nn.softplus(-(w0 + jnp.einsum('btzr,zrc->btzc', jnp.tanh(wd), w2))) - 0.5
    decay = jnp.exp(-jnp.exp(wlog))
    icl = jax.nn.sigmoid(a0 + jnp.einsum('btzr,zrc->btzc', ad, a2))
    g = jax.nn.sigmoid(gd) @ g2.astype(jnp.float32)
    heads = lambda z: z.reshape(z.shape[:-1] + (C_HEADS, C_HEAD))
    kk = heads(k * k_k)
    kk = kk / jnp.maximum(jnp.sqrt(jnp.sum(kk * kk, axis=-1, keepdims=True)), 1e-12)
    k_dir = k[:, :, None] * (1 + (icl - 1) * k_a)
    return heads(r), heads(k_dir), heads(v), heads(decay), heads(icl), kk, g


def rwkv_direction(inp, z):
    r, kd, v, dec, icl, kk, g = inp
    return r, dec[:, :, z], kd[:, :, z], v, -kk, kk * icl[:, :, z]


def wkv7_scan(state0, r, w, k, v, a, b, reverse, emit):
    xs = tuple(jnp.moveaxis(t_, 1, 0) for t_ in (r, w, k, v, a, b))

    def step(S, inp):
        r_t, w_t, k_t, v_t, a_t, b_t = inp
        sa = jnp.einsum('bhvk,bhk->bhv', S, a_t)
        S = S * w_t[:, :, None, :] + sa[..., None] * b_t[:, :, None, :] + v_t[..., None] * k_t[:, :, None, :]
        y = jnp.einsum('bhvk,bhk->bhv', S, r_t) if emit else None
        return S, y

    S, ys = lax.scan(step, state0, xs, reverse=reverse)
    return S, (jnp.moveaxis(ys, 0, 1) if emit else None)


def rwkv_readout(y_dirs, inp, r_k, ln_w, ln_b):
    r, kd, v, dec, icl, kk, g = inp
    y = y_dirs[0] + y_dirs[1]
    mean = jnp.mean(y, axis=-1, keepdims=True)
    var = jnp.mean(jnp.square(y - mean), axis=-1, keepdims=True)
    y = (y - mean) * lax.rsqrt(var + C_GN_EPS) * ln_w.reshape(C_HEADS, C_HEAD) + ln_b.reshape(C_HEADS, C_HEAD)
    bonus = jnp.sum(jnp.sum(r[:, :, None] * kd * r_k, axis=-1, keepdims=True) * v[:, :, None], axis=2)
    bsz, t = y.shape[:2]
    return (y + bonus).reshape(bsz, t, C_WIDTH) * g


def routed_experts(t, experts, gates, w1, w3, w2):
    n_tok, d = t.shape
    n_assign = n_tok * TOP_K
    n_exp = w1.shape[0]
    n_blocks = (n_assign + n_exp * (EXPERT_BLOCK - 1) + EXPERT_BLOCK - 1) // EXPERT_BLOCK
    n_rows = n_blocks * EXPERT_BLOCK
    flat_e = experts.reshape(-1)
    order = jnp.argsort(flat_e)
    sorted_e = flat_e[order]
    counts = jnp.bincount(flat_e, length=n_exp)
    padded = (counts + EXPERT_BLOCK - 1) // EXPERT_BLOCK * EXPERT_BLOCK
    pad_end = jnp.cumsum(padded)
    pad_start = pad_end - padded
    raw_start = jnp.cumsum(counts) - counts
    dest = pad_start[sorted_e] + jnp.arange(n_assign, dtype=jnp.int32) - raw_start[sorted_e]
    row_tok = jnp.full((n_rows,), n_tok, jnp.int32).at[dest].set((order // TOP_K).astype(jnp.int32))
    row_gate = jnp.zeros((n_rows,), jnp.float32).at[dest].set(gates.reshape(-1)[order])
    block_exp = jnp.minimum(jnp.searchsorted(pad_end, jnp.arange(n_blocks, dtype=jnp.int32) * EXPERT_BLOCK, side='right'), n_exp - 1)
    t_pad = jnp.concatenate([t, jnp.zeros((1, d), t.dtype)], axis=0)

    def expert_block(args):
        rows, e = args
        xb = t_pad[rows]
        h = jax.nn.silu(xb @ w1[e]) * (xb @ w3[e])
        return h @ w2[e]

    y_rows = lax.map(expert_block, (row_tok.reshape(n_blocks, EXPERT_BLOCK), block_exp))
    y_rows = y_rows.reshape(n_rows, d) * row_gate[:, None].astype(t.dtype)
    out = jnp.zeros((n_tok + 1, d), t.dtype).at[row_tok].add(y_rows)
    return out[:n_tok]


def hier_moe(t, w_group, b_group, w_expert, b_expert, w1, w3, w2):
    n_tok = t.shape[0]
    tf = t.astype(jnp.float32)
    group_prob = jax.nn.softmax(tf @ w_group.astype(jnp.float32) + b_group, axis=-1)
    g_idx = jnp.argmax(group_prob, axis=-1).astype(jnp.int32)
    g_p = jnp.max(group_prob, axis=-1)
    e_logits = (tf @ w_expert.astype(jnp.float32) + b_expert).reshape(n_tok, N_GROUPS, EXPERTS_PER_GROUP)
    e_logits = jnp.take_along_axis(e_logits, g_idx[:, None, None], axis=1)[:, 0]
    top_v, top_i = lax.top_k(e_logits, TOP_K)
    gates = jax.nn.softmax(top_v, axis=-1) * g_p[:, None]
    experts = g_idx[:, None] * EXPERTS_PER_GROUP + top_i.astype(jnp.int32)
    return routed_experts(t, experts, gates, w1, w3, w2)


def setup_inputs(seed: int = 0) -> dict:
    key = jax.random.key(seed)
    ks = iter(jax.random.split(key, 40))
    f32 = jnp.float32
    nrm = lambda shape, s: jax.random.normal(next(ks), shape, f32) * s
    D = D_MODEL
    return {
        "x": nrm((BATCH, SEQ, D), 1.0),
        "c": nrm((BATCH, D), 1.0),
        "ctx": nrm((BATCH, CTX_LEN, D), 1.0),
        "c_ctx": nrm((D,), 1.0),
        "w_mod": nrm((DEPTH, D, 6 * D), 0.5 * D ** -0.5),
        "b_mod": nrm((DEPTH, 6 * D), 0.02),
        "g_mix": 1.0 + nrm((DEPTH, D), 0.05),
        "g_ffn": 1.0 + nrm((DEPTH, D), 0.05),
        "w_in": nrm((DEPTH, D, IN_WIDTH), D ** -0.5),
        "a_q_gain": 1.0 + nrm((DEPTH, HEAD_DIM), 0.05),
        "a_k_gain": 1.0 + nrm((DEPTH, HEAD_DIM), 0.05),
        "b_lq1": nrm((DEPTH, B_QK_DIM), 0.1),
        "b_lk1": nrm((DEPTH, B_QK_DIM), 0.1),
        "b_lq2": nrm((DEPTH, B_QK_DIM), 0.1),
        "b_lk2": nrm((DEPTH, B_QK_DIM), 0.1),
        "b_subln": 1.0 + nrm((DEPTH, B_V_DIM), 0.05),
        "c_mu": jax.random.uniform(next(ks), (DEPTH, C_SLICE), f32),
        "c_w0": jax.random.uniform(next(ks), (DEPTH, N_DIR, C_WIDTH), f32, -6.0, -1.0),
        "c_w2": nrm((DEPTH, N_DIR, C_DECAY_LORA, C_WIDTH), 0.1),
        "c_a0": nrm((DEPTH, N_DIR, C_WIDTH), 0.5),
        "c_a2": nrm((DEPTH, N_DIR, C_ICL_LORA, C_WIDTH), 0.1),
        "c_g2": nrm((DEPTH, C_GATE_LORA, C_WIDTH), C_GATE_LORA ** -0.5),
        "c_k_k": 0.85 + nrm((DEPTH, C_WIDTH), 0.05),
        "c_k_a": 1.0 + nrm((DEPTH, C_WIDTH), 0.05),
        "c_r_k": nrm((DEPTH, C_HEADS, C_HEAD), 0.1),
        "c_ln_w": 1.0 + nrm((DEPTH, C_WIDTH), 0.05),
        "c_ln_b": nrm((DEPTH, C_WIDTH), 0.01),
        "w_out": nrm((DEPTH, MIX_WIDTH, D), MIX_WIDTH ** -0.5),
        "r_w_group": nrm((DEPTH, D, N_GROUPS), D ** -0.5),
        "r_b_group": nrm((DEPTH, N_GROUPS), 0.01),
        "r_w_expert": nrm((DEPTH, D, N_EXPERTS), D ** -0.5),
        "r_b_expert": nrm((DEPTH, N_EXPERTS), 0.01),
        "e_w1": nrm((DEPTH, N_EXPERTS, D, EXPERT_FF), D ** -0.5),
        "e_w3": nrm((DEPTH, N_EXPERTS, D, EXPERT_FF), D ** -0.5),
        "e_w2": nrm((DEPTH, N_EXPERTS, EXPERT_FF, D), EXPERT_FF ** -0.5),
        "g_final": 1.0 + nrm((D,), 0.05),
    }


def reference(x, c, ctx, c_ctx, w_mod, b_mod, g_mix, g_ffn, w_in, a_q_gain, a_k_gain,
              b_lq1, b_lk1, b_lq2, b_lk2, b_subln, c_mu, c_w0, c_w2, c_a0, c_a2, c_g2,
              c_k_k, c_k_a, c_r_k, c_ln_w, c_ln_b, w_out, r_w_group, r_b_group,
              r_w_expert, r_b_expert, e_w1, e_w3, e_w2, g_final):
    bsz, n_lat, d = x.shape
    n_ctx = ctx.shape[1]
    ROWS = n_lat // GRID_W
    row_idx = jnp.repeat(jnp.arange(ROWS, dtype=jnp.int32), GRID_W)
    col_idx = jnp.tile(jnp.arange(GRID_W, dtype=jnp.int32), ROWS)
    cos_a, sin_a = axial_rope_tables(row_idx, col_idx, HEAD_DIM)
    cos_a, sin_a = cos_a[None, :, None, :], sin_a[None, :, None, :]
    cos_b, sin_b = axial_rope_tables(row_idx, col_idx, B_QK_DIM)
    cos_b, sin_b = cos_b[None, :, None, None, :], sin_b[None, :, None, None, :]
    act_c = jax.nn.silu(c)
    act_cc = jax.nn.silu(c_ctx)
    state0 = jnp.zeros((bsz, C_HEADS, C_HEAD, C_HEAD), jnp.float32)
    xc = ctx
    for i in range(DEPTH):
        update_ctx = i < DEPTH - 1
        mod_l = (act_c @ w_mod[i] + b_mod[i]).reshape(bsz, 6, 1, d)
        sh1, sc1, ga1, sh2, sc2, ga2 = [mod_l[:, j] for j in range(6)]
        mod_c = (act_cc @ w_mod[i] + b_mod[i]).reshape(6, d)
        csh1, csc1, cga1, csh2, csc2, cga2 = [mod_c[j] for j in range(6)]

        p_l = ada_norm(x, g_mix[i], sh1, sc1) @ w_in[i]
        p_c = ada_norm(xc, g_mix[i], csh1, csc1) @ w_in[i]
        aq_l, ak_l, av_l, bq_l, bk_l, bv_l, cp_l = split_combined(p_l)
        aq_c, ak_c, av_c, bq_c, bk_c, bv_c, cp_c = split_combined(p_c)

        aq_l = apply_rope(rms_norm(aq_l, a_q_gain[i]), cos_a, sin_a)
        ak_l = apply_rope(rms_norm(ak_l, a_k_gain[i]), cos_a, sin_a)
        ak_c = rms_norm(ak_c, a_k_gain[i])
        ak_all = jnp.concatenate([ak_c, ak_l], axis=1)
        av_all = jnp.concatenate([av_c, av_l], axis=1)
        ya_l = sweep_query_blocks(gqa_attend, aq_l, ak_all, av_all)

        lam_init = 0.8 - 0.6 * math.exp(-0.3 * i)
        lam = (jnp.exp(jnp.sum(b_lq1[i].astype(jnp.float32) * b_lk1[i].astype(jnp.float32)))
               - jnp.exp(jnp.sum(b_lq2[i].astype(jnp.float32) * b_lk2[i].astype(jnp.float32))) + lam_init)
        bq_l = apply_rope(bq_l, cos_b, sin_b)
        bk_l = apply_rope(bk_l, cos_b, sin_b)
        bk_all = jnp.concatenate([bk_c, bk_l], axis=1)
        bv_all = jnp.concatenate([bv_c, bv_l], axis=1)
        yb_l = diff_head_norm(sweep_query_blocks(diff_attend, bq_l, bk_all, bv_all, lam), b_subln[i], lam_init)

        rw = (c_mu[i], c_w0[i], c_w2[i], c_a0[i], c_a2[i], c_g2[i], c_k_k[i], c_k_a[i])
        cin_l = rwkv_inputs(cp_l, *rw)
        cin_c = rwkv_inputs(cp_c, *rw)
        yl_dirs, yc_dirs = [], []
        for z, rev in enumerate((False, True)):
            s_c, y_c = wkv7_scan(state0, *rwkv_direction(cin_c, z), reverse=rev, emit=update_ctx)
            _, y_l = wkv7_scan(s_c, *rwkv_direction(cin_l, z), reverse=rev, emit=True)
            yl_dirs.append(y_l)
            yc_dirs.append(y_c)
        yc_l = rwkv_readout(yl_dirs, cin_l, c_r_k[i], c_ln_w[i], c_ln_b[i]).astype(x.dtype)

        if update_ctx:
            ya_c = gqa_attend(rms_norm(aq_c, a_q_gain[i]), ak_c, av_c)
            yb_c = diff_head_norm(diff_attend(bq_c, bk_c, bv_c, lam), b_subln[i], lam_init)
            yc_c = rwkv_readout(yc_dirs, cin_c, c_r_k[i], c_ln_w[i], c_ln_b[i]).astype(xc.dtype)
            xc = xc + cga1 * (jnp.concatenate([ya_c, yb_c, yc_c], axis=-1) @ w_out[i])
        x = x + ga1 * (jnp.concatenate([ya_l, yb_l, yc_l], axis=-1) @ w_out[i])

        f_l = ada_norm(x, g_ffn[i], sh2, sc2).reshape(bsz * n_lat, d)
        if update_ctx:
            f_c = ada_norm(xc, g_ffn[i], csh2, csc2).reshape(bsz * n_ctx, d)
            tokens = jnp.concatenate([f_l, f_c], axis=0)
        else:
            tokens = f_l
        y = hier_moe(tokens, r_w_group[i], r_b_group[i], r_w_expert[i], r_b_expert[i], e_w1[i], e_w3[i], e_w2[i])
        x = x + ga2 * y[:bsz * n_lat].reshape(bsz, n_lat, d)
        if update_ctx:
            xc = xc + cga2 * y[bsz * n_lat:].reshape(bsz, n_ctx, d)
    return rms_norm(x, g_final)
```

```python
import functools
import math

import jax
import jax.numpy as jnp
from jax import lax
from jax.experimental import pallas as pl
from jax.experimental.pallas import tpu as pltpu

F32 = jnp.float32
BF16 = jnp.bfloat16

GRID_W = 64
HEAD_DIM = 64
ROPE_THETA = 10000.0
NORM_EPS = 1e-6
A_HEADS = 4
A_KV_HEADS = 2
B_HEADS = 4
B_QK_DIM = 32
B_V_DIM = 64
C_HEADS = 8
C_HEAD = 64
C_WIDTH = C_HEADS * C_HEAD
C_DECAY_LORA = 64
C_ICL_LORA = 64
C_GATE_LORA = 128
N_DIR = 2
C_GN_EPS = 64e-5
A_Q = A_HEADS * HEAD_DIM
A_KV = A_KV_HEADS * HEAD_DIM
B_Q = B_HEADS * 2 * B_QK_DIM
B_V = B_HEADS * B_V_DIM
C_SLICE = 3 * C_WIDTH + N_DIR * C_DECAY_LORA + N_DIR * C_ICL_LORA + C_GATE_LORA
AB_WIDTH = A_Q + 2 * A_KV + 2 * B_Q + B_V
IN_WIDTH = AB_WIDTH + C_SLICE
N_GROUPS = 4
EXPERTS_PER_GROUP = 8
N_EXPERTS = N_GROUPS * EXPERTS_PER_GROUP
TOP_K = 2

LANES = 128
ROW_BLOCK = 256
HALO_ROWS = 8
WKV_CHUNK = 64
WKV_GROUP = 4
WKV_LANES = WKV_GROUP * C_HEAD
EXPERT_ROWS = 256
ROUTE_LANES = LANES
VMEM_LIMIT = 48 * 1024 * 1024

_NN = (((1,), (0,)), ((), ()))
_NT = (((1,), (1,)), ((), ()))
_TN = (((0,), (0,)), ((), ()))


def _dg(a, b, dims=_NN, precision=None):
    return lax.dot_general(a, b, dims, precision=precision, preferred_element_type=F32)


def _mm_bf16(a, b, dims=_NN):
    return _dg(a.astype(BF16), b.astype(BF16), dims)


def _mm_f32(a, b, dims=_NN):
    return _dg(a, b, dims, lax.Precision.HIGHEST)


def _split3(x):
    hi = x.astype(BF16)
    r1 = x - hi.astype(F32)
    mid = r1.astype(BF16)
    lo = (r1 - mid.astype(F32)).astype(BF16)
    return hi, mid, lo


def _mm_ones_rhs(x, ones_bf16):
    hi, mid, lo = _split3(x)
    return _dg(hi, ones_bf16) + _dg(mid, ones_bf16) + _dg(lo, ones_bf16)


def _mm_ones_lhs(ones_bf16, x):
    hi, mid, lo = _split3(x)
    return _dg(ones_bf16, hi) + _dg(ones_bf16, mid) + _dg(ones_bf16, lo)


def _head_sums(x, bd_ref):
    bd = bd_ref[...]
    tiles = [_mm_ones_rhs(x[:, j:j + LANES], bd) for j in range(0, x.shape[1], LANES)]
    return tiles[0] if len(tiles) == 1 else jnp.concatenate(tiles, axis=1)


def _sigmoid(x):
    return 1.0 / (1.0 + jnp.exp(-x))


def _silu(x):
    return x * _sigmoid(x)


def _params(semantics):
    return pltpu.CompilerParams(dimension_semantics=semantics, vmem_limit_bytes=VMEM_LIMIT)


def _mod_kernel(c_ref, w_ref, b_ref, o_ref):
    a = _silu(c_ref[...])
    o_ref[...] = _mm_bf16(a, w_ref[...]) + b_ref[...]


def _mod_call(cc, w_mod, b_mod):
    depth, d, n = w_mod.shape
    rows = cc.shape[0]
    tn = 1024
    return pl.pallas_call(
        _mod_kernel,
        grid=(depth, n // tn),
        in_specs=[
            pl.BlockSpec((rows, d), lambda l, j: (0, 0)),
            pl.BlockSpec((None, d, tn), lambda l, j: (l, 0, j)),
            pl.BlockSpec((None, 1, tn), lambda l, j: (l, 0, j)),
        ],
        out_specs=pl.BlockSpec((None, rows, tn), lambda l, j: (l, 0, j)),
        out_shape=jax.ShapeDtypeStruct((depth, rows, n), F32),
        compiler_params=_params(("parallel", "parallel")),
        name="mod_proj",
    )(cc, w_mod, b_mod.reshape(depth, 1, n))


def _rope(x, cos, sin_up, sin_dn, quarter):
    tiles = []
    for j in range(0, x.shape[1], LANES):
        xt = x[:, j:j + LANES]
        tiles.append(xt * cos + pltpu.roll(xt, LANES - quarter, 1) * sin_up + pltpu.roll(xt, quarter, 1) * sin_dn)
    return tiles[0] if len(tiles) == 1 else jnp.concatenate(tiles, axis=1)


def _inproj_kernel(x_ref, mod_ref, g_ref, w_ref, qg_ref, kg_ref, ra_ref, rb_ref, bd_ref,
                   aq_ref, ak_ref, av_ref, bq_ref, bk_ref, bv_ref, cp_ref):
    x = x_ref[...]
    ms = jnp.mean(x * x, axis=-1, keepdims=True)
    y = x * lax.rsqrt(ms + NORM_EPS) * g_ref[...]
    h = (y * (1.0 + mod_ref[1:2, :]) + mod_ref[0:1, :]).astype(BF16)

    ca, sua, sda = ra_ref[0], ra_ref[1], ra_ref[2]
    cb, sub, sdb = rb_ref[0], rb_ref[1], rb_ref[2]

    pa = _dg(h, w_ref[:, 0:A_Q + 2 * A_KV])
    aq = pa[:, 0:A_Q]
    aq = aq * lax.rsqrt(_head_sums(aq * aq, bd_ref) * (1.0 / HEAD_DIM) + NORM_EPS) * qg_ref[...]
    aq = _rope(aq, ca, sua, sda, HEAD_DIM // 4) * (HEAD_DIM ** -0.5)
    aq_ref[...] = aq.astype(BF16)
    ak = pa[:, A_Q:A_Q + A_KV]
    ak = ak * lax.rsqrt(_head_sums(ak * ak, bd_ref) * (1.0 / HEAD_DIM) + NORM_EPS) * kg_ref[...]
    ak_ref[...] = _rope(ak, ca, sua, sda, HEAD_DIM // 4).astype(BF16)
    av_ref[...] = pa[:, A_Q + A_KV:].astype(BF16)

    o = A_Q + 2 * A_KV
    pb = _dg(h, w_ref[:, o:o + 2 * B_Q + B_V])
    bq = _rope(pb[:, 0:B_Q], cb, sub, sdb, B_QK_DIM // 4) * (B_QK_DIM ** -0.5)
    bq_ref[...] = bq.astype(BF16)
    bk_ref[...] = _rope(pb[:, B_Q:2 * B_Q], cb, sub, sdb, B_QK_DIM // 4).astype(BF16)
    bv_ref[...] = pb[:, 2 * B_Q:].astype(BF16)

    cp_ref[...] = _dg(h, w_ref[:, AB_WIDTH:])


def _inproj_call(x_all, mod, g_mix, w_in_bf16, qgain, kgain, rope_a, rope_b, bd, n_ctx):
    b, t, d = x_all.shape
    nb = t // ROW_BLOCK
    ncb = n_ctx // ROW_BLOCK
    row = lambda w: pl.BlockSpec((None, ROW_BLOCK, w), lambda bi, i: (bi, i, 0))
    full2 = lambda a: pl.BlockSpec(a.shape, lambda bi, i: (0,) * a.ndim)
    outs = [(A_Q, BF16), (A_KV, BF16), (A_KV, BF16), (B_Q, BF16), (B_Q, BF16), (B_V, BF16), (C_SLICE, F32)]
    return pl.pallas_call(
        _inproj_kernel,
        grid=(b, nb),
        in_specs=[
            row(d),
            pl.BlockSpec((None, None, 6, d), lambda bi, i: (bi, jnp.where(i < ncb, 0, 1), 0, 0)),
            full2(g_mix), full2(w_in_bf16), full2(qgain), full2(kgain),
            pl.BlockSpec((3, ROW_BLOCK, LANES), lambda bi, i: (0, i, 0)),
            pl.BlockSpec((3, ROW_BLOCK, LANES), lambda bi, i: (0, i, 0)),
            full2(bd),
        ],
        out_specs=[row(w) for w, _ in outs],
        out_shape=[jax.ShapeDtypeStruct((b, t, w), dt) for w, dt in outs],
        compiler_params=_params(("parallel", "parallel")),
        name="in_proj",
    )(x_all, mod, g_mix, w_in_bf16, qgain, kgain, rope_a, rope_b, bd)


def _softmax_parts(s):
    m = jnp.max(s, axis=-1, keepdims=True)
    e = jnp.exp(s - m)
    return e, 1.0 / jnp.sum(e, axis=-1, keepdims=True)


def _attn_body(kv_len, lam, lam_scale, aq_ref, ak_ref, av_ref, bq_ref, bk_ref, bv_ref, sub_ref, o_ref):
    for kvh in range(A_KV_HEADS):
        k = ak_ref[0:kv_len, kvh * HEAD_DIM:(kvh + 1) * HEAD_DIM]
        v = av_ref[0:kv_len, kvh * HEAD_DIM:(kvh + 1) * HEAD_DIM]
        for g in range(A_HEADS // A_KV_HEADS):
            hd = kvh * (A_HEADS // A_KV_HEADS) + g
            q = aq_ref[:, hd * HEAD_DIM:(hd + 1) * HEAD_DIM]
            e, inv = _softmax_parts(_dg(q, k, _NT))
            o = _dg(e.astype(BF16), v) * inv
            o_ref[:, hd * HEAD_DIM:(hd + 1) * HEAD_DIM] = o.astype(o_ref.dtype)
    lane = lax.broadcasted_iota(jnp.int32, (1, 2 * B_QK_DIM), 1)
    for hd in range(B_HEADS):
        q = bq_ref[:, hd * 2 * B_QK_DIM:(hd + 1) * 2 * B_QK_DIM]
        k = bk_ref[0:kv_len, hd * 2 * B_QK_DIM:(hd + 1) * 2 * B_QK_DIM]
        v = bv_ref[0:kv_len, hd * B_V_DIM:(hd + 1) * B_V_DIM]
        zero = jnp.zeros_like(q)
        e0, i0 = _softmax_parts(_dg(jnp.where(lane < B_QK_DIM, q, zero), k, _NT))
        e1, i1 = _softmax_parts(_dg(jnp.where(lane >= B_QK_DIM, q, zero), k, _NT))
        a = e0 * i0 - e1 * (i1 * lam)
        o = _dg(a.astype(BF16), v)
        ms = jnp.mean(o * o, axis=-1, keepdims=True)
        o = o * lax.rsqrt(ms + NORM_EPS) * sub_ref[...] * lam_scale
        o_ref[:, A_Q + hd * B_V_DIM:A_Q + (hd + 1) * B_V_DIM] = o.astype(o_ref.dtype)


def _attn_kernel(lam_ref, aq_ref, ak_ref, av_ref, bq_ref, bk_ref, bv_ref, sub_ref, o_ref, *, n_ctx, lam_scale):
    lam = lam_ref[0]
    args = (aq_ref, ak_ref, av_ref, bq_ref, bk_ref, bv_ref, sub_ref, o_ref)
    is_ctx = pl.program_id(1) < n_ctx // ROW_BLOCK

    @pl.when(is_ctx)
    def _():
        _attn_body(n_ctx, lam, lam_scale, *args)

    @pl.when(jnp.logical_not(is_ctx))
    def _():
        _attn_body(ak_ref.shape[0], lam, lam_scale, *args)


def _attn_call(lam, aq, ak, av, bq, bk, bv, subln, n_ctx, lam_scale):
    b, t, _ = aq.shape
    nb = t // ROW_BLOCK
    row = lambda w: pl.BlockSpec((None, ROW_BLOCK, w), lambda bi, i: (bi, i, 0))
    seq = lambda w: pl.BlockSpec((None, t, w), lambda bi, i: (bi, 0, 0))
    return pl.pallas_call(
        functools.partial(_attn_kernel, n_ctx=n_ctx, lam_scale=lam_scale),
        grid=(b, nb),
        in_specs=[
            pl.BlockSpec(memory_space=pltpu.SMEM),
            row(A_Q), seq(A_KV), seq(A_KV), row(B_Q), seq(B_Q), seq(B_V),
            pl.BlockSpec(subln.shape, lambda bi, i: (0, 0)),
        ],
        out_specs=row(A_Q + B_V),
        out_shape=jax.ShapeDtypeStruct((b, t, A_Q + B_V), BF16),
        compiler_params=_params(("parallel", "parallel")),
        name="attention",
    )(lam, aq, ak, av, bq, bk, bv, subln)


def _softplus(u):
    return jnp.maximum(u, 0.0) + jnp.log(1.0 + jnp.exp(-jnp.abs(u)))


def _rwkv_in_kernel(cp_ref, prev_ref, next_ref, mu_ref, w0_ref, w2_ref, a0_ref, a2_ref, g2_ref,
                    kk_ref_w, ka_ref, rk_ref, bd_ref,
                    r_ref, v_ref, kk_ref, g_ref, bonus_ref, lw_ref, kd_ref, b_ref, *, n_ctx_blocks):
    i = pl.program_id(1)
    nb = pl.num_programs(1)
    p = cp_ref[...]
    rows = p.shape[0]
    seg_start = jnp.logical_or(i == 0, i == n_ctx_blocks)
    seg_end = jnp.logical_or(i == n_ctx_blocks - 1, i == nb - 1)
    halo_prev = prev_ref[HALO_ROWS - 1:HALO_ROWS, :] * jnp.where(seg_start, 0.0, 1.0)
    halo_next = next_ref[0:1, :] * jnp.where(seg_end, 0.0, 1.0)
    ridx = lax.broadcasted_iota(jnp.int32, (rows, 1), 0)
    prev = jnp.where(ridx == 0, halo_prev, pltpu.roll(p, 1, 0))
    nxt = jnp.where(ridx == rows - 1, halo_next, pltpu.roll(p, rows - 1, 0))
    pf = p + mu_ref[...] * (0.5 * (prev + nxt) - p)

    o1, o2, o3 = C_WIDTH, 2 * C_WIDTH, 3 * C_WIDTH
    o4 = o3 + N_DIR * C_DECAY_LORA
    o5 = o4 + N_DIR * C_ICL_LORA
    r = pf[:, 0:o1]
    k = pf[:, o1:o2]
    v = pf[:, o2:o3]
    wd = pf[:, o3:o4]
    ad = pf[:, o4:o5]
    gd = pf[:, o5:]

    wlog = -_softplus(-(w0_ref[...] + _mm_f32(jnp.tanh(wd), w2_ref[...]))) - 0.5
    lw = -jnp.exp(wlog)
    icl = _sigmoid(a0_ref[...] + _mm_f32(ad, a2_ref[...]))
    g = _mm_f32(_sigmoid(gd), g2_ref[...])

    kkr = k * kk_ref_w[...]
    kk = kkr / jnp.maximum(jnp.sqrt(_head_sums(kkr * kkr, bd_ref)), 1e-12)
    ka = ka_ref[...]
    kd0 = k * (1.0 + (icl[:, 0:C_WIDTH] - 1.0) * ka)
    kd1 = k * (1.0 + (icl[:, C_WIDTH:] - 1.0) * ka)
    bonus = _head_sums(r * (kd0 + kd1) * rk_ref[...], bd_ref) * v

    r_ref[...] = r
    v_ref[...] = v
    kk_ref[...] = kk
    g_ref[...] = g
    bonus_ref[...] = bonus
    lw_ref[0] = lw[:, 0:C_WIDTH]
    lw_ref[1] = lw[:, C_WIDTH:]
    kd_ref[0] = kd0
    kd_ref[1] = kd1
    b_ref[0] = kk * icl[:, 0:C_WIDTH]
    b_ref[1] = kk * icl[:, C_WIDTH:]


def _rwkv_in_call(cp, mu, w0, w2bd, a0, a2bd, g2, k_k, k_a, r_k, bd, n_ctx):
    b, t, cs = cp.shape
    nb = t // ROW_BLOCK
    hpb = ROW_BLOCK // HALO_ROWS
    n_halo = t // HALO_ROWS
    full = lambda a: pl.BlockSpec(a.shape, lambda bi, i: (0,) * a.ndim)
    row = pl.BlockSpec((None, ROW_BLOCK, C_WIDTH), lambda bi, i: (bi, i, 0))
    row2 = pl.BlockSpec((N_DIR, None, ROW_BLOCK, C_WIDTH), lambda bi, i: (0, bi, i, 0))
    one = jax.ShapeDtypeStruct((b, t, C_WIDTH), F32)
    two = jax.ShapeDtypeStruct((N_DIR, b, t, C_WIDTH), F32)
    return pl.pallas_call(
        functools.partial(_rwkv_in_kernel, n_ctx_blocks=n_ctx // ROW_BLOCK),
        grid=(b, nb),
        in_specs=[
            pl.BlockSpec((None, ROW_BLOCK, cs), lambda bi, i: (bi, i, 0)),
            pl.BlockSpec((None, HALO_ROWS, cs), lambda bi, i: (bi, jnp.maximum(i * hpb - 1, 0), 0)),
            pl.BlockSpec((None, HALO_ROWS, cs), lambda bi, i: (bi, jnp.minimum((i + 1) * hpb, n_halo - 1), 0)),
            full(mu), full(w0), full(w2bd), full(a0), full(a2bd), full(g2), full(k_k), full(k_a), full(r_k), full(bd),
        ],
        out_specs=[row, row, row, row, row, row2, row2, row2],
        out_shape=[one, one, one, one, one, two, two, two],
        compiler_params=_params(("parallel", "parallel")),
        name="rwkv_inputs",
    )(cp, cp, cp, mu, w0, w2bd, a0, a2bd, g2, k_k, k_a, r_k, bd)


def _wkv_chunk(lw, r, k, v, kk, b, s, ms, tri_incl, m_strict, m_incl, bdm, last_row):
    ln = lw.shape[0]
    grp = ms.shape[0] // ln
    c = _mm_ones_lhs(tri_incl, lw)
    eg = jnp.exp(c)
    egi = jnp.exp(-c)
    rt = r * eg
    kt = k * egi
    bt = b * egi
    at = -kk * jnp.exp(c - lw)
    gl = eg[last_row:last_row + 1, :]

    tile = lambda x: jnp.concatenate([x] * grp, axis=0)
    p = jnp.concatenate([tile(at) * ms, tile(rt) * ms], axis=0)
    rhs = jnp.concatenate([tile(bt), tile(kt)], axis=0)
    gln = grp * ln
    q1 = _mm_f32(p, rhs, _NT)
    nmat = q1[0:gln, 0:gln] * m_strict
    aak = q1[0:gln, gln:] * m_strict
    rb = q1[gln:, 0:gln] * m_incl
    rk = q1[gln:, gln:] * m_incl
    q2 = _mm_f32(p, s, _NT)
    msv = tile(v) * ms
    x = q2[0:gln] + _mm_f32(aak, msv)
    x = x + _mm_f32(nmat, x)
    npow = nmat
    for _ in range(int(math.log2(ln)) - 1):
        npow = _mm_f32(npow, npow)
        x = x + _mm_f32(npow, x)
    yms = q2[gln:] + _mm_f32(jnp.concatenate([rb, rk], axis=1), jnp.concatenate([x, msv], axis=0))
    fold = lambda z: functools.reduce(lambda u, w_: u + w_, [z[g * ln:(g + 1) * ln] for g in range(grp)])
    y = fold(yms)
    u = jnp.concatenate([fold(x), v], axis=0)
    wm = jnp.concatenate([bt * gl, kt * gl], axis=0)
    s_new = s * gl + _mm_f32(u, wm, _TN) * bdm
    return y, s_new


def _wkv_kernel(ms_ref, tri_ref, msk_ref, bdm_ref,
                lwf, rf, kf, vf, kkf, bf, lwr, rr, kr, vr, kkr, br,
                yf_ref, yr_ref, s_ref):
    @pl.when(pl.program_id(2) == 0)
    def _():
        s_ref[...] = jnp.zeros_like(s_ref)

    ln = lwf.shape[0]
    ms = ms_ref[...]
    bdm = bdm_ref[...]
    yf, sf = _wkv_chunk(lwf[...], rf[...], kf[...], vf[...], kkf[...], bf[...], s_ref[0],
                        ms, tri_ref[0], msk_ref[0], msk_ref[1], bdm, ln - 1)
    yr, sr = _wkv_chunk(lwr[...], rr[...], kr[...], vr[...], kkr[...], br[...], s_ref[1],
                        ms, tri_ref[1], msk_ref[2], msk_ref[3], bdm, 0)
    yf_ref[...] = yf
    yr_ref[...] = yr
    s_ref[0] = sf
    s_ref[1] = sr


def _wkv_consts():
    ln, grp = WKV_CHUNK, WKV_GROUP
    gln, w = grp * ln, WKV_LANES
    row = jnp.arange(gln)[:, None]
    col = jnp.arange(gln)[None, :]
    same = (row // ln) == (col // ln)
    masks = jnp.stack([same & (col < row), same & (col <= row), same & (col > row), same & (col >= row)]).astype(F32)
    t_r = jnp.arange(ln)[:, None]
    t_c = jnp.arange(ln)[None, :]
    tri = jnp.stack([t_c <= t_r, t_c >= t_r]).astype(BF16)
    ms = ((row // ln) == (jnp.arange(w)[None, :] // C_HEAD)).astype(F32)
    wi = jnp.arange(w)
    bdm = ((wi[:, None] // C_HEAD) == (wi[None, :] // C_HEAD)).astype(F32)
    return ms, tri, masks, bdm


def _wkv_call(lw2, kd2, b2, r, v, kk, n_ctx):
    _, b, t, c = lw2.shape
    ln, w = WKV_CHUNK, WKV_LANES
    nch = t // ln
    ncc = n_ctx // ln
    ms, tri, masks, bdm = _wkv_consts()
    rev = lambda s: jnp.where(s < ncc, ncc - 1 - s, nch - 1 - (s - ncc))
    full = lambda a: pl.BlockSpec(a.shape, lambda bi, h, s: (0,) * a.ndim)
    f1 = pl.BlockSpec((None, ln, w), lambda bi, h, s: (bi, s, h))
    r1 = pl.BlockSpec((None, ln, w), lambda bi, h, s: (bi, rev(s), h))
    f2 = pl.BlockSpec((None, None, ln, w), lambda bi, h, s: (0, bi, s, h))
    r2 = pl.BlockSpec((None, None, ln, w), lambda bi, h, s: (1, bi, rev(s), h))
    out = jax.ShapeDtypeStruct((b, t, c), F32)
    return pl.pallas_call(
        _wkv_kernel,
        grid=(b, c // w, nch),
        in_specs=[full(ms), full(tri), full(masks), full(bdm),
                  f2, f1, f2, f1, f1, f2,
                  r2, r1, r2, r1, r1, r2],
        out_specs=[f1, r1],
        out_shape=[out, out],
        scratch_shapes=[pltpu.VMEM((N_DIR, w, w), F32)],
        compiler_params=_params(("parallel", "parallel", "arbitrary")),
        name="wkv_scan",
    )(ms, tri, masks, bdm, lw2, r, kd2, v, kk, b2, lw2, r, kd2, v, kk, b2)


def _route(logits):
    lane = lax.broadcasted_iota(jnp.int32, logits.shape, 1)
    neg = jnp.float32(-jnp.inf)
    big = jnp.int32(ROUTE_LANES)
    is_g = lane < N_GROUPS
    gl = jnp.where(is_g, logits, neg)
    gmax = jnp.max(gl, axis=-1, keepdims=True)
    g_idx = jnp.min(jnp.where(gl == gmax, lane, big), axis=-1, keepdims=True)
    g_p = 1.0 / jnp.sum(jnp.where(is_g, jnp.exp(logits - gmax), 0.0), axis=-1, keepdims=True)
    lo = N_GROUPS + g_idx * EXPERTS_PER_GROUP
    el = jnp.where(jnp.logical_and(lane >= lo, lane < lo + EXPERTS_PER_GROUP), logits, neg)
    t1 = jnp.max(el, axis=-1, keepdims=True)
    i1 = jnp.min(jnp.where(el == t1, lane, big), axis=-1, keepdims=True)
    el2 = jnp.where(lane == i1, neg, el)
    t2 = jnp.max(el2, axis=-1, keepdims=True)
    i2 = jnp.min(jnp.where(el2 == t2, lane, big), axis=-1, keepdims=True)
    e2 = jnp.exp(t2 - t1)
    w1 = g_p / (1.0 + e2)
    w2 = g_p * e2 / (1.0 + e2)
    out = jnp.where(lane == 0, (i1 - N_GROUPS).astype(F32), 0.0)
    out = jnp.where(lane == 1, (i2 - N_GROUPS).astype(F32), out)
    out = jnp.where(lane == 2, w1, out)
    return jnp.where(lane == 3, w2, out)


def _mix_out_kernel(yf_ref, yr_ref, bonus_ref, g_ref, yab_ref, x_ref, mod_ref, wo_ref, lnw_ref, lnb_ref,
                    gffn_ref, wr_ref, br_ref, bd_ref, xo_ref, f_ref, route_ref):
    y = yf_ref[...] + yr_ref[...]
    mean = _head_sums(y, bd_ref) * (1.0 / C_HEAD)
    dlt = y - mean
    var = _head_sums(dlt * dlt, bd_ref) * (1.0 / C_HEAD)
    yn = dlt * lax.rsqrt(var + C_GN_EPS) * lnw_ref[...] + lnb_ref[...]
    yc = ((yn + bonus_ref[...]) * g_ref[...]).astype(BF16)
    nab = yab_ref.shape[1]
    mix = _dg(yab_ref[...], wo_ref[0:nab, :]) + _dg(yc, wo_ref[nab:, :])
    x = x_ref[...] + mod_ref[2:3, :] * mix
    xo_ref[...] = x
    ms = jnp.mean(x * x, axis=-1, keepdims=True)
    f = x * lax.rsqrt(ms + NORM_EPS) * gffn_ref[...] * (1.0 + mod_ref[4:5, :]) + mod_ref[3:4, :]
    f_ref[...] = f
    route_ref[...] = _route(_mm_f32(f, wr_ref[...]) + br_ref[...])


def _mix_out_call(yf, yr, bonus, g, yab, x_all, mod, w_out_bf16, ln_w, ln_b, g_ffn, w_route, b_route, bd, n_ctx):
    b, t, d = x_all.shape
    nb = t // ROW_BLOCK
    ncb = n_ctx // ROW_BLOCK
    row = lambda w: pl.BlockSpec((None, ROW_BLOCK, w), lambda bi, i: (bi, i, 0))
    full = lambda a: pl.BlockSpec(a.shape, lambda bi, i: (0,) * a.ndim)
    return pl.pallas_call(
        _mix_out_kernel,
        grid=(b, nb),
        in_specs=[
            row(C_WIDTH), row(C_WIDTH), row(C_WIDTH), row(C_WIDTH), row(A_Q + B_V), row(d),
            pl.BlockSpec((None, None, 6, d), lambda bi, i: (bi, jnp.where(i < ncb, 0, 1), 0, 0)),
            full(w_out_bf16), full(ln_w), full(ln_b), full(g_ffn), full(w_route), full(b_route), full(bd),
        ],
        out_specs=[row(d), row(d), row(ROUTE_LANES)],
        out_shape=[jax.ShapeDtypeStruct((b, t, d), F32), jax.ShapeDtypeStruct((b, t, d), F32),
                   jax.ShapeDtypeStruct((b, t, ROUTE_LANES), F32)],
        compiler_params=_params(("parallel", "parallel")),
        name="mix_out",
    )(yf, yr, bonus, g, yab, x_all, mod, w_out_bf16, ln_w, ln_b, g_ffn, w_route, b_route, bd)


def _row_copy(src_hbm, dst_vmem, sem, src_row, dst_row):
    return pltpu.make_async_copy(src_hbm.at[pl.ds(src_row, 1)], dst_vmem.at[pl.ds(dst_row, 1)], sem)


def _gather_rows(idx_vmem_ref, idx_smem, src_hbm, dst_vmem, sem_idx, sem_rows):
    n = idx_smem.shape[-1]
    cp = pltpu.make_async_copy(idx_vmem_ref, idx_smem, sem_idx)
    cp.start()
    cp.wait()

    def start(j, carry):
        _row_copy(src_hbm, dst_vmem, sem_rows, idx_smem[0, 0, j], j).start()
        return carry

    lax.fori_loop(0, n, start, 0, unroll=8)

    def wait(j, carry):
        _row_copy(src_hbm, dst_vmem, sem_rows, 0, j).wait()
        return carry

    lax.fori_loop(0, n, wait, 0, unroll=8)


def _expert_kernel(bexp_ref, nused_ref, tok_ref, gate_ref, f_hbm, w1_ref, w3_ref, w2_ref, y_ref,
                   xbuf, idx_smem, sem_idx, sem_rows):
    i = pl.program_id(0)

    @pl.when(i < nused_ref[0])
    def _():
        _gather_rows(tok_ref, idx_smem, f_hbm, xbuf, sem_idx, sem_rows)
        xb = xbuf[...].astype(BF16)
        a = _dg(xb, w1_ref[...].astype(BF16))
        bb = _dg(xb, w3_ref[...].astype(BF16))
        h = (_silu(a) * bb).astype(BF16)
        y_ref[...] = _dg(h, w2_ref[...].astype(BF16)) * gate_ref[...]

    @pl.when(i >= nused_ref[0])
    def _():
        y_ref[...] = jnp.zeros_like(y_ref)


def _expert_call(block_exp, n_used, row_tok, row_gate, f2d, w1, w3, w2):
    nblk = row_tok.shape[0]
    d = f2d.shape[1]
    ff = w1.shape[2]
    grid_spec = pltpu.PrefetchScalarGridSpec(
        num_scalar_prefetch=2,
        grid=(nblk,),
        in_specs=[
            pl.BlockSpec((1, 1, EXPERT_ROWS), lambda i, be, nu: (i, 0, 0)),
            pl.BlockSpec((EXPERT_ROWS, 1), lambda i, be, nu: (i, 0)),
            pl.BlockSpec(memory_space=pl.ANY),
            pl.BlockSpec((None, d, ff), lambda i, be, nu: (be[i], 0, 0)),
            pl.BlockSpec((None, d, ff), lambda i, be, nu: (be[i], 0, 0)),
            pl.BlockSpec((None, ff, d), lambda i, be, nu: (be[i], 0, 0)),
        ],
        out_specs=pl.BlockSpec((EXPERT_ROWS, d), lambda i, be, nu: (i, 0)),
        scratch_shapes=[
            pltpu.VMEM((EXPERT_ROWS, d), F32),
            pltpu.SMEM((1, 1, EXPERT_ROWS), jnp.int32),
            pltpu.SemaphoreType.DMA(()),
            pltpu.SemaphoreType.DMA(()),
        ],
    )
    return pl.pallas_call(
        _expert_kernel,
        grid_spec=grid_spec,
        out_shape=jax.ShapeDtypeStruct((nblk * EXPERT_ROWS, d), F32),
        compiler_params=_params(("arbitrary",)),
        name="experts",
    )(block_exp, n_used, row_tok, row_gate, f2d, w1, w3, w2)


def _combine_kernel(pos_ref, y_hbm, x_ref, mod_ref, o_ref, buf, idx_smem, sem_idx, sem_rows):
    _gather_rows(pos_ref, idx_smem, y_hbm, buf, sem_idx, sem_rows)
    n = x_ref.shape[0]
    o_ref[...] = x_ref[...] + mod_ref[5:6, :] * (buf[0:n, :] + buf[n:, :])


def _combine_call(pos, y_rows, x2d, mod, blocks_per_sample, n_ctx_blocks):
    n, d = x2d.shape
    nblk = n // ROW_BLOCK
    bps = blocks_per_sample
    return pl.pallas_call(
        _combine_kernel,
        grid=(nblk,),
        in_specs=[
            pl.BlockSpec((1, 1, TOP_K * ROW_BLOCK), lambda i: (i, 0, 0)),
            pl.BlockSpec(memory_space=pl.ANY),
            pl.BlockSpec((ROW_BLOCK, d), lambda i: (i, 0)),
            pl.BlockSpec((None, None, 6, d), lambda i: (i // bps, jnp.where(i % bps < n_ctx_blocks, 0, 1), 0, 0)),
        ],
        out_specs=pl.BlockSpec((ROW_BLOCK, d), lambda i: (i, 0)),
        out_shape=jax.ShapeDtypeStruct((n, d), F32),
        scratch_shapes=[
            pltpu.VMEM((TOP_K * ROW_BLOCK, d), F32),
            pltpu.SMEM((1, 1, TOP_K * ROW_BLOCK), jnp.int32),
            pltpu.SemaphoreType.DMA(()),
            pltpu.SemaphoreType.DMA(()),
        ],
        compiler_params=_params(("arbitrary",)),
        name="moe_combine",
    )(pos, y_rows, x2d, mod)


def _dispatch_plan(route2d):
    n = route2d.shape[0]
    experts = route2d[:, 0:TOP_K].astype(jnp.int32)
    gates = route2d[:, TOP_K:2 * TOP_K]
    n_assign = n * TOP_K
    nblk = (n_assign + N_EXPERTS * (EXPERT_ROWS - 1) + EXPERT_ROWS - 1) // EXPERT_ROWS
    flat_e = experts.reshape(-1)
    onehot = (flat_e[:, None] == jnp.arange(N_EXPERTS, dtype=jnp.int32)[None, :]).astype(jnp.int32)
    csum = jnp.cumsum(onehot, axis=0)
    rank = jnp.sum(csum * onehot, axis=1) - 1
    counts = csum[-1]
    padded = (counts + EXPERT_ROWS - 1) // EXPERT_ROWS * EXPERT_ROWS
    pad_end = jnp.cumsum(padded)
    pad_start = pad_end - padded
    dest = (pad_start[flat_e] + rank).astype(jnp.int32)
    tok = jnp.arange(n_assign, dtype=jnp.int32) // TOP_K
    row_tok = jnp.zeros((nblk * EXPERT_ROWS,), jnp.int32).at[dest].set(tok)
    row_gate = jnp.zeros((nblk * EXPERT_ROWS,), F32).at[dest].set(gates.reshape(-1))
    starts = jnp.arange(nblk, dtype=jnp.int32) * EXPERT_ROWS
    block_exp = jnp.minimum(jnp.searchsorted(pad_end, starts, side="right"), N_EXPERTS - 1).astype(jnp.int32)
    n_used = (pad_end[-1] // EXPERT_ROWS).astype(jnp.int32).reshape(1)
    pos = dest.reshape(n // ROW_BLOCK, ROW_BLOCK, TOP_K).transpose(0, 2, 1).reshape(n // ROW_BLOCK, 1, TOP_K * ROW_BLOCK)
    return (block_exp, n_used, row_tok.reshape(nblk, 1, EXPERT_ROWS), row_gate.reshape(nblk * EXPERT_ROWS, 1), pos)


def _final_norm_kernel(x_ref, g_ref, o_ref):
    x = x_ref[...]
    ms = jnp.mean(x * x, axis=-1, keepdims=True)
    o_ref[...] = x * lax.rsqrt(ms + NORM_EPS) * g_ref[...]


def _final_norm_call(x_all, g_final, n_ctx):
    b, t, d = x_all.shape
    ncb = n_ctx // ROW_BLOCK
    nb = (t - n_ctx) // ROW_BLOCK
    return pl.pallas_call(
        _final_norm_kernel,
        grid=(b, nb),
        in_specs=[pl.BlockSpec((None, ROW_BLOCK, d), lambda bi, i: (bi, i + ncb, 0)),
                  pl.BlockSpec((1, d), lambda bi, i: (0, 0))],
        out_specs=pl.BlockSpec((None, ROW_BLOCK, d), lambda bi, i: (bi, i, 0)),
        out_shape=jax.ShapeDtypeStruct((b, t - n_ctx, d), F32),
        compiler_params=_params(("parallel", "parallel")),
        name="final_norm",
    )(x_all, g_final.reshape(1, d))


def _rope_tables(n_ctx, n_lat, dim):
    half = dim // 2
    quarter = dim // 4
    inv = ROPE_THETA ** (-jnp.arange(0, half, 2, dtype=F32) / half)
    tpos = jnp.arange(n_lat, dtype=jnp.int32)
    ar = (tpos // GRID_W).astype(F32)[:, None] * inv
    ac = (tpos % GRID_W).astype(F32)[:, None] * inv
    ang = jnp.concatenate([ar, ar, ac, ac], axis=-1)
    cos = jnp.concatenate([jnp.ones((n_ctx, dim), F32), jnp.cos(ang)], axis=0)
    sin = jnp.concatenate([jnp.zeros((n_ctx, dim), F32), jnp.sin(ang)], axis=0)
    first = (jnp.arange(dim) % half) < quarter
    sin_up = jnp.where(first[None, :], -sin, 0.0)
    sin_dn = jnp.where(first[None, :], 0.0, sin)
    reps = LANES // dim
    return jnp.stack([jnp.tile(cos, (1, reps)), jnp.tile(sin_up, (1, reps)), jnp.tile(sin_dn, (1, reps))])


def _dir_block_diag(w):
    nd, r, c = w.shape
    out = jnp.zeros((nd * r, nd * c), w.dtype)
    for z in range(nd):
        out = out.at[z * r:(z + 1) * r, z * c:(z + 1) * c].set(w[z])
    return out


def kernel(x, c, ctx, c_ctx, w_mod, b_mod, g_mix, g_ffn, w_in, a_q_gain, a_k_gain, b_lq1, b_lk1, b_lq2, b_lk2, b_subln, c_mu, c_w0, c_w2, c_a0, c_a2, c_g2, c_k_k, c_k_a, c_r_k, c_ln_w, c_ln_b, w_out, r_w_group, r_b_group, r_w_expert, r_b_expert, e_w1, e_w3, e_w2, g_final):
    bsz, n_lat, d = x.shape
    n_ctx = ctx.shape[1]
    depth = w_mod.shape[0]
    t = n_ctx + n_lat
    assert n_ctx % ROW_BLOCK == 0 and n_lat % ROW_BLOCK == 0 and n_ctx % WKV_CHUNK == 0
    assert n_lat % GRID_W == 0 and d % LANES == 0

    x_all = jnp.concatenate([ctx, x], axis=1)
    rope_a = _rope_tables(n_ctx, n_lat, HEAD_DIM)
    rope_b = _rope_tables(n_ctx, n_lat, B_QK_DIM)
    li = jnp.arange(LANES)
    bd = ((li[:, None] // C_HEAD) == (li[None, :] // C_HEAD)).astype(BF16)

    pad_rows = -(bsz + 1) % 8
    cc = jnp.concatenate([c, c_ctx[None, :], jnp.zeros((pad_rows, d), F32)], axis=0)
    mod_all = _mod_call(cc, w_mod, b_mod)
    mod_lat = mod_all[:, :bsz].reshape(depth, bsz, 1, 6, d)
    mod_ctx = jnp.broadcast_to(mod_all[:, bsz].reshape(depth, 1, 1, 6, d), (depth, bsz, 1, 6, d))
    mod = jnp.concatenate([mod_ctx, mod_lat], axis=2)

    w_route = jnp.concatenate(
        [r_w_group, r_w_expert, jnp.zeros((depth, d, ROUTE_LANES - N_GROUPS - N_EXPERTS), F32)], axis=-1)
    b_route = jnp.concatenate(
        [r_b_group, r_b_expert, jnp.zeros((depth, ROUTE_LANES - N_GROUPS - N_EXPERTS), F32)], axis=-1)

    for i in range(depth):
        lam_init = 0.8 - 0.6 * math.exp(-0.3 * i)
        lam = (jnp.exp(jnp.sum(b_lq1[i] * b_lk1[i])) - jnp.exp(jnp.sum(b_lq2[i] * b_lk2[i])) + lam_init)
        aq, ak, av, bq, bk, bv, cp = _inproj_call(
            x_all, mod[i], g_mix[i].reshape(1, d), w_in[i].astype(BF16),
            jnp.tile(a_q_gain[i], LANES // HEAD_DIM * (A_Q // LANES)).reshape(1, A_Q),
            jnp.tile(a_k_gain[i], LANES // HEAD_DIM * (A_KV // LANES)).reshape(1, A_KV),
            rope_a, rope_b, bd, n_ctx)
        yab = _attn_call(lam.reshape(1).astype(F32), aq, ak, av, bq, bk, bv,
                         b_subln[i].reshape(1, B_V_DIM), n_ctx, 1.0 - lam_init)
        r, v, kk, g, bonus, lw2, kd2, b2 = _rwkv_in_call(
            cp, c_mu[i].reshape(1, C_SLICE), c_w0[i].reshape(1, N_DIR * C_WIDTH), _dir_block_diag(c_w2[i]),
            c_a0[i].reshape(1, N_DIR * C_WIDTH), _dir_block_diag(c_a2[i]), c_g2[i],
            c_k_k[i].reshape(1, C_WIDTH), c_k_a[i].reshape(1, C_WIDTH), c_r_k[i].reshape(1, C_WIDTH), bd, n_ctx)
        yf, yr = _wkv_call(lw2, kd2, b2, r, v, kk, n_ctx)
        x_mid, f, route = _mix_out_call(
            yf, yr, bonus, g, yab, x_all, mod[i], w_out[i].astype(BF16),
            c_ln_w[i].reshape(1, C_WIDTH), c_ln_b[i].reshape(1, C_WIDTH), g_ffn[i].reshape(1, d),
            w_route[i], b_route[i].reshape(1, ROUTE_LANES), bd, n_ctx)
        block_exp, n_used, row_tok, row_gate, pos = _dispatch_plan(route.reshape(bsz * t, ROUTE_LANES))
        y_rows = _expert_call(block_exp, n_used, row_tok, row_gate, f.reshape(bsz * t, d), e_w1[i], e_w3[i], e_w2[i])
        x_all = _combine_call(pos, y_rows, x_mid.reshape(bsz * t, d), mod[i], t // ROW_BLOCK,
                              n_ctx // ROW_BLOCK).reshape(bsz, t, d)
    return _final_norm_call(x_all, g_final, n_ctx)
```
